```python
import math
import jax, jax.numpy as jnp
from jax import lax
import numpy as np

D_MODEL = 4096
BATCH = 4
SEQ = 2048
DEPTH = 4
DEC_BATCH = 8
DEC_SEQ = 1
PAST_LEN = 8192
PAGE_SIZE = 128

N_HEADS = 16
HEAD_DIM = 128
ATTN_WIDTH = N_HEADS * HEAD_DIM
IDX_HEADS = 32
IDX_DIM = 64
TOPK_MAX = 256
Q_BLOCK = 32
D_CONV = 2048
CONV_WIDTH = 31
N_EXPERTS = 16
N_GROUPS = 4
EXPERTS_PER_GROUP = N_EXPERTS // N_GROUPS
TOP_K_EXPERTS = 2
GROUP_SCORE_K = 2
D_EXPERT = 1024
ROPE_THETA = 10000.0
LN_EPS = 1e-5
DN_ALPHA = (2 * DEPTH) ** 0.25
DN_BETA = (8 * DEPTH) ** -0.25
SPLITS = (ATTN_WIDTH, ATTN_WIDTH, ATTN_WIDTH, IDX_HEADS * IDX_DIM, IDX_DIM, IDX_HEADS, 2 * D_CONV, 2 * D_MODEL)
IN_WIDTH = 3 * ATTN_WIDTH + IDX_HEADS * IDX_DIM + IDX_DIM + IDX_HEADS + 2 * D_CONV + 2 * D_MODEL

kernel_name = "dsa_conformer_gated_moe_decoder_step"


def layer_norm(x, g, b):
    xf = x.astype(jnp.float32)
    mu = xf.mean(-1, keepdims=True)
    var = jnp.square(xf - mu).mean(-1, keepdims=True)
    y = (xf - mu) * lax.rsqrt(var + LN_EPS) * g.astype(jnp.float32) + b.astype(jnp.float32)
    return y.astype(x.dtype)


def rope(x, pos):
    d = x.shape[-1]
    inv = ROPE_THETA ** (-jnp.arange(0, d, 2, dtype=jnp.float32) / d)
    ang = pos.astype(jnp.float32)[:, None] * inv[None, :]
    cos, sin = jnp.cos(ang)[:, None, :], jnp.sin(ang)[:, None, :]
    x1, x2 = jnp.split(x.astype(jnp.float32), 2, axis=-1)
    return jnp.concatenate([x1 * cos - x2 * sin, x1 * sin + x2 * cos], axis=-1).astype(x.dtype)


def split_in_proj(h):
    points, acc = [], 0
    for s in SPLITS[:-1]:
        acc += s
        points.append(acc)
    return jnp.split(h, points, axis=-1)


def mixer_inputs(x, w_in, pos):
    B, T, _ = x.shape
    q, k, v, iq, ik, iw, glu, gates = split_in_proj(x @ w_in)
    q = rope(q.reshape(B, T, N_HEADS, HEAD_DIM), pos)
    k = rope(k.reshape(B, T, N_HEADS, HEAD_DIM), pos)
    v = v.reshape(B, T, N_HEADS, HEAD_DIM)
    iq = rope(iq.reshape(B, T, IDX_HEADS, IDX_DIM), pos)
    ik = rope(ik[:, :, None, :], pos)[:, :, 0, :]
    return q, k, v, iq, ik, iw, glu, gates


def gather_rows(rows, idx):
    return jax.vmap(lambda r, i: r[i])(rows, idx)


def indexer_scores(iq, iw, ik, q_pos, k_pos):
    dots = jnp.einsum('bqhd,bkd->bqhk', iq, ik, preferred_element_type=jnp.float32) * (IDX_DIM ** -0.5)
    score = jnp.einsum('bqhk,bqh->bqk', jax.nn.relu(dots), iw.astype(jnp.float32) * (IDX_HEADS ** -0.5))
    causal = k_pos[None, :] <= q_pos[:, None]
    return jnp.where(causal[None], score, -jnp.inf)


def sparse_attend(q, ks, vs, valid):
    logits = jnp.einsum('bqhd,bqkhd->bqhk', q, ks, preferred_element_type=jnp.float32) * (HEAD_DIM ** -0.5)
    logits = jnp.where(valid[:, :, None, :], logits, -jnp.inf)
    p = jax.nn.softmax(logits, axis=-1)
    return jnp.einsum('bqhk,bqkhd->bqhd', p.astype(vs.dtype), vs)


def prompt_attention(q, k, v, iq, iw, ik):
    B, T = q.shape[0], q.shape[1]
    n_sel = min(TOPK_MAX, T // 4)
    k_pos = jnp.arange(T)

    def block(i):
        start = i * Q_BLOCK
        sl = lambda a: lax.dynamic_slice_in_dim(a, start, Q_BLOCK, axis=1)
        q_pos = start + jnp.arange(Q_BLOCK)
        score = indexer_scores(sl(iq), sl(iw), ik, q_pos, k_pos)
        _, idx = lax.top_k(score, n_sel)
        valid = idx <= q_pos[None, :, None]
        return sparse_attend(sl(q), gather_rows(k, idx), gather_rows(v, idx), valid)

    out = lax.map(block, jnp.arange(T // Q_BLOCK))
    return jnp.moveaxis(out, 0, 1).reshape(B, T, ATTN_WIDTH)


def sample_attention(q, k, v, iq, iw, ik, ck, cv, cik, page_table):
    DB, DS = q.shape[0], q.shape[1]
    L = PAST_LEN + DS
    n_sel = min(TOPK_MAX, L // 4)
    ik_past = cik[page_table].reshape(DB, PAST_LEN, IDX_DIM)
    ik_all = jnp.concatenate([ik_past, ik], axis=1)
    q_pos = PAST_LEN + jnp.arange(DS)
    score = indexer_scores(iq, iw, ik_all, q_pos, jnp.arange(L))
    _, idx = lax.top_k(score, n_sel)
    valid = idx <= q_pos[None, :, None]
    past_idx = jnp.minimum(idx, PAST_LEN - 1)
    phys = jax.vmap(lambda pt, ix: pt[ix])(page_table, past_idx // PAGE_SIZE)
    slot = past_idx % PAGE_SIZE
    is_new = (idx >= PAST_LEN)[..., None, None]
    new_idx = jnp.clip(idx - PAST_LEN, 0, DS - 1)
    ks = jnp.where(is_new, gather_rows(k, new_idx), ck[phys, slot])
    vs = jnp.where(is_new, gather_rows(v, new_idx), cv[phys, slot])
    return sparse_attend(q, ks, vs, valid).reshape(DB, DS, ATTN_WIDTH)


def conv_branch(glu, prefix, w_dw, b_dw, g, b):
    a, gt = jnp.split(glu, 2, axis=-1)
    u = a * jax.nn.sigmoid(gt)
    ext = jnp.concatenate([prefix, u], axis=1)
    y = lax.conv_general_dilated(ext, w_dw[:, None, :], window_strides=(1,), padding='VALID',
                                 dimension_numbers=('NWC', 'WIO', 'NWC'),
                                 feature_group_count=D_CONV) + b_dw
    y = jax.nn.silu(layer_norm(y, g, b))
    return y, ext[:, -(CONV_WIDTH - 1):]


def moe_ffn(x, w_router, b_router, w_gate, w_up, w_down):
    probs = jax.nn.softmax((x @ w_router).astype(jnp.float32), axis=-1)
    sel = probs + b_router.astype(jnp.float32)
    grouped = sel.reshape(sel.shape[:-1] + (N_GROUPS, EXPERTS_PER_GROUP))
    group_score = lax.top_k(grouped, GROUP_SCORE_K)[0].sum(-1)
    best = jnp.argmax(group_score, axis=-1)
    in_group = (jnp.arange(N_EXPERTS) // EXPERTS_PER_GROUP) == best[..., None]
    _, top_idx = lax.top_k(jnp.where(in_group, sel, -jnp.inf), TOP_K_EXPERTS)
    top_w = jnp.take_along_axis(probs, top_idx, axis=-1)
    top_w = top_w / top_w.sum(-1, keepdims=True)
    gate = jnp.sum(jax.nn.one_hot(top_idx, N_EXPERTS, dtype=jnp.float32) * top_w[..., None], axis=-2)
    h = jax.nn.silu(jnp.einsum('btd,edf->btef', x, w_gate)) * jnp.einsum('btd,edf->btef', x, w_up)
    return jnp.einsum('btef,efd->btd', h * gate.astype(x.dtype)[..., None], w_down)


def finish_layer(x, attn, conv, gates, w_attn_out, w_conv_out, w_out,
                 ln1_g, ln1_b, ln2_g, ln2_b, w_router, b_router, w_gate, w_up, w_down):
    ga, gb = jnp.split(gates, 2, axis=-1)
    merged = jax.nn.sigmoid(ga) * (attn @ w_attn_out) + jax.nn.sigmoid(gb) * (conv @ w_conv_out)
    x = layer_norm(DN_ALPHA * x + merged @ w_out, ln1_g, ln1_b)
    return layer_norm(DN_ALPHA * x + moe_ffn(x, w_router, b_router, w_gate, w_up, w_down), ln2_g, ln2_b)


def setup_inputs(seed: int = 0) -> dict:
    key = jax.random.key(seed)
    ks = jax.random.split(key, 24)
    f32 = jnp.float32
    n_pages = PAST_LEN // PAGE_SIZE
    n_used = DEC_BATCH * n_pages
    n_pool = n_used + n_used // 4
    nrm = lambda k, shape, s: jax.random.normal(k, shape, f32) * s
    col_scale = jnp.concatenate([jnp.ones((2 * ATTN_WIDTH,), f32), jnp.full((ATTN_WIDTH,), DN_BETA, f32),
                                 jnp.ones((IN_WIDTH - 3 * ATTN_WIDTH,), f32)])
    return {
        "x_prompt": nrm(ks[0], (BATCH, SEQ, D_MODEL), 1.0),
        "x_sample": nrm(ks[1], (DEC_BATCH, DEC_SEQ, D_MODEL), 1.0),
        "cache_k": nrm(ks[2], (DEPTH, n_pool, PAGE_SIZE, N_HEADS, HEAD_DIM), 1.0),
        "cache_v": nrm(ks[3], (DEPTH, n_pool, PAGE_SIZE, N_HEADS, HEAD_DIM), 1.0),
        "cache_idx_k": nrm(ks[4], (DEPTH, n_pool, PAGE_SIZE, IDX_DIM), 1.0),
        "state_conv": nrm(ks[5], (DEPTH, DEC_BATCH, CONV_WIDTH - 1, D_CONV), 0.5),
        "page_table": jax.random.permutation(ks[6], n_pool)[:n_used].reshape(DEC_BATCH, n_pages).astype(jnp.int32),
        "w_in": nrm(ks[7], (DEPTH, D_MODEL, IN_WIDTH), D_MODEL ** -0.5) * col_scale,
        "w_dw": nrm(ks[8], (DEPTH, CONV_WIDTH, D_CONV), CONV_WIDTH ** -0.5),
        "b_dw": nrm(ks[9], (DEPTH, D_CONV), 0.02),
        "conv_ln_g": 1.0 + nrm(ks[10], (DEPTH, D_CONV), 0.02),
        "conv_ln_b": nrm(ks[11], (DEPTH, D_CONV), 0.02),
        "w_attn_out": nrm(ks[12], (DEPTH, ATTN_WIDTH, D_MODEL), DN_BETA * ATTN_WIDTH ** -0.5),
        "w_conv_out": nrm(ks[13], (DEPTH, D_CONV, D_MODEL), DN_BETA * D_CONV ** -0.5),
        "w_out": nrm(ks[14], (DEPTH, D_MODEL, D_MODEL), DN_BETA * D_MODEL ** -0.5),
        "ln1_g": 1.0 + nrm(ks[15], (DEPTH, D_MODEL), 0.02),
        "ln1_b": nrm(ks[16], (DEPTH, D_MODEL), 0.02),
        "ln2_g": 1.0 + nrm(ks[17], (DEPTH, D_MODEL), 0.02),
        "ln2_b": nrm(ks[18], (DEPTH, D_MODEL), 0.02),
        "w_router": nrm(ks[19], (D_MODEL, N_EXPERTS), D_MODEL ** -0.5),
        "b_router": nrm(ks[20], (N_EXPERTS,), 0.01),
        "w_gate": nrm(ks[21], (DEPTH, N_EXPERTS, D_MODEL, D_EXPERT), D_MODEL ** -0.5),
        "w_up": nrm(ks[22], (DEPTH, N_EXPERTS, D_MODEL, D_EXPERT), D_MODEL ** -0.5),
        "w_down": nrm(ks[23], (DEPTH, N_EXPERTS, D_EXPERT, D_MODEL), DN_BETA * D_EXPERT ** -0.5),
    }


def reference(x_prompt, x_sample, cache_k, cache_v, cache_idx_k, state_conv, page_table,
              w_in, w_dw, b_dw, conv_ln_g, conv_ln_b, w_attn_out, w_conv_out, w_out,
              ln1_g, ln1_b, ln2_g, ln2_b, w_router, b_router, w_gate, w_up, w_down):
    B, T, _ = x_prompt.shape
    DB, DS, _ = x_sample.shape
    pos_p = jnp.arange(T)
    pos_s = PAST_LEN + jnp.arange(DS)
    xp, xs = x_prompt, x_sample
    kp, vp, ikp, cp, ksm, vsm, iks, cs = [], [], [], [], [], [], [], []
    for l in range(DEPTH):
        shared = (w_attn_out[l], w_conv_out[l], w_out[l], ln1_g[l], ln1_b[l], ln2_g[l], ln2_b[l],
                  w_router, b_router, w_gate[l], w_up[l], w_down[l])
        q, k, v, iq, ik, iw, glu, gates = mixer_inputs(xp, w_in[l], pos_p)
        attn = prompt_attention(q, k, v, iq, iw, ik)
        zeros = jnp.zeros((B, CONV_WIDTH - 1, D_CONV), xp.dtype)
        conv, tail = conv_branch(glu, zeros, w_dw[l], b_dw[l], conv_ln_g[l], conv_ln_b[l])
        xp = finish_layer(xp, attn, conv, gates, *shared)
        kp.append(k.reshape(B * T // PAGE_SIZE, PAGE_SIZE, N_HEADS, HEAD_DIM))
        vp.append(v.reshape(B * T // PAGE_SIZE, PAGE_SIZE, N_HEADS, HEAD_DIM))
        ikp.append(ik.reshape(B * T // PAGE_SIZE, PAGE_SIZE, IDX_DIM))
        cp.append(tail)
        q, k, v, iq, ik, iw, glu, gates = mixer_inputs(xs, w_in[l], pos_s)
        attn = sample_attention(q, k, v, iq, iw, ik, cache_k[l], cache_v[l], cache_idx_k[l], page_table)
        conv, tail = conv_branch(glu, state_conv[l], w_dw[l], b_dw[l], conv_ln_g[l], conv_ln_b[l])
        xs = finish_layer(xs, attn, conv, gates, *shared)
        ksm.append(k)
        vsm.append(v)
        iks.append(ik)
        cs.append(tail)
    y_prompt, y_sample = xp, xs
    new_k_prompt, new_v_prompt = jnp.stack(kp), jnp.stack(vp)
    new_idx_k_prompt, new_conv_prompt = jnp.stack(ikp), jnp.stack(cp)
    new_k_sample, new_v_sample = jnp.stack(ksm), jnp.stack(vsm)
    new_idx_k_sample, new_conv_sample = jnp.stack(iks), jnp.stack(cs)
    return (y_prompt, y_sample, new_k_prompt, new_v_prompt, new_idx_k_prompt, new_conv_prompt,
            new_k_sample, new_v_sample, new_idx_k_sample, new_conv_sample)
```

```python
import functools

import jax
import jax.numpy as jnp
from jax import lax
from jax.experimental import pallas as pl
from jax.experimental.pallas import tpu as pltpu

D_MODEL = 4096
BATCH = 4
SEQ = 2048
DEPTH = 4
DEC_BATCH = 8
PAST_LEN = 8192
PAGE_SIZE = 128
N_PAGES = PAST_LEN // PAGE_SIZE
N_HEADS = 16
HEAD_DIM = 128
ATTN_WIDTH = N_HEADS * HEAD_DIM
IDX_HEADS = 32
IDX_DIM = 64
N_SEL = 256
D_CONV = 2048
CONV_WIDTH = 31
N_EXPERTS = 16
N_GROUPS = 4
D_EXPERT = 1024
ROPE_THETA = 10000.0
LN_EPS = 1e-5
DN_ALPHA = (2 * DEPTH) ** 0.25

F32 = jnp.float32
BF = jnp.bfloat16
LANES = 128
VMEM_LIMIT = 56 * 1024 * 1024
NEG = -1e30
INT_MIN = -2 ** 31
NT = (((1,), (1,)), ((), ()))

OFF_Q, OFF_K, OFF_V, OFF_IQ, OFF_GA, OFF_GG, OFF_GTA, OFF_GTB, OFF_IKW = (
    0, 2048, 4096, 6144, 8192, 10240, 12288, 16384, 20480)
W_CAT = 20608

M_PROMPT = BATCH * SEQ
M_SAMPLE = 128


def _cp(*sem):
    return pltpu.CompilerParams(dimension_semantics=sem, vmem_limit_bytes=VMEM_LIMIT)


def _sds(shape, dtype):
    return jax.ShapeDtypeStruct(shape, dtype)


def _sigmoid(x):
    return 1.0 / (1.0 + jnp.exp(-x))


def _layer_norm(z, g, b):
    mu = jnp.mean(z, axis=-1, keepdims=True)
    zc = z - mu
    var = jnp.mean(zc * zc, axis=-1, keepdims=True)
    return zc * lax.rsqrt(var + LN_EPS) * g + b


def mm_call(xs, ws, extras, outs, epilogue, *, tm, ncol):
    m = xs[0].shape[0]
    nx, nw, ne = len(xs), len(ws), len(extras)
    xidx = [xi for (_, xi, _) in ws]

    def body(*refs):
        x_refs, w_refs = refs[:nx], refs[nx:nx + nw]
        e_refs, o_refs = refs[nx + nw:nx + nw + ne], refs[nx + nw + ne:]
        dots = [jnp.dot(x_refs[xi][...], w[...], preferred_element_type=F32) for xi, w in zip(xidx, w_refs)]
        epilogue(dots, e_refs, o_refs)

    in_specs = [pl.BlockSpec((tm, x.shape[1]), lambda i, j: (i, 0)) for x in xs]
    in_specs += [s for (_, _, s) in ws] + [s for (_, s) in extras]
    res = pl.pallas_call(
        body, grid=(m // tm, ncol), in_specs=in_specs,
        out_specs=[s for (_, s) in outs], out_shape=[sd for (sd, _) in outs],
        compiler_params=_cp("parallel", "arbitrary"),
    )(*xs, *[w for (w, _, _) in ws], *[e for (e, _) in extras])
    return res


def _wspec(k, tn, l, off_blocks):
    return pl.BlockSpec((None, k, tn), lambda i, j: (l, 0, off_blocks + j))


def _ospec(tm, tn):
    return pl.BlockSpec((tm, tn), lambda i, j: (i, j))


def _rowspec(tm, n):
    return pl.BlockSpec((tm, n), lambda i, j: (i, 0))


def _rot_half(x, half):
    if 2 * half == LANES:
        return pltpu.roll(x, half, 1)
    lane = lax.broadcasted_iota(jnp.int32, x.shape, 1)
    return jnp.where((lane % (2 * half)) < half, pltpu.roll(x, LANES - half, 1), pltpu.roll(x, half, 1))


def _epi_rope(half):
    def epi(dots, e_refs, o_refs):
        d, c, s = dots[0], e_refs[0][...], e_refs[1][...]
        for ci in range(d.shape[1] // LANES):
            x = d[:, ci * LANES:(ci + 1) * LANES]
            y = x * c + _rot_half(x, half) * s
            for o in o_refs:
                o[:, ci * LANES:(ci + 1) * LANES] = y.astype(o.dtype)
    return epi


def _epi_plain(dots, e_refs, o_refs):
    for o in o_refs:
        o[...] = dots[0].astype(o.dtype)


def _epi_ikw(dots, e_refs, o_refs):
    d, c, s = dots[0], e_refs[0][...], e_refs[1][...]
    y = d * c + _rot_half(d, IDX_DIM // 2) * s
    o_refs[0][...] = y[:, :IDX_DIM]
    o_refs[1][...] = y[:, :IDX_DIM].astype(BF)
    o_refs[2][...] = d[:, IDX_DIM:IDX_DIM + IDX_HEADS]


def _epi_glu(dots, e_refs, o_refs):
    o_refs[0][...] = dots[0] * _sigmoid(dots[1])


def _epi_merge(dots, e_refs, o_refs):
    o_refs[0][...] = (_sigmoid(dots[0]) * dots[2] + _sigmoid(dots[1]) * dots[3]).astype(BF)


def _epi_moe_up(nb):
    def epi(dots, e_refs, o_refs):
        e = pl.program_id(1) // nb
        g = e_refs[0][...]
        lane = lax.broadcasted_iota(jnp.int32, g.shape, 1)
        gcol = jnp.sum(jnp.where(lane == e, g, 0.0), axis=1, keepdims=True)
        a = dots[0]
        o_refs[0][...] = ((a * _sigmoid(a)) * dots[1] * gcol).astype(BF)
    return epi


def in_proj(xb, l, wcat, tabs, tm):
    m = xb.shape[0]
    c128, s128, c64, s64 = tabs
    tn = 512
    nb = 2048 // tn
    k = D_MODEL
    r128 = _rowspec(tm, LANES)
    q_b, = mm_call([xb], [(wcat, 0, _wspec(k, tn, l, OFF_Q // tn))], [(c128, r128), (s128, r128)],
                   [(_sds((m, 2048), BF), _ospec(tm, tn))], _epi_rope(HEAD_DIM // 2), tm=tm, ncol=nb)
    k_f, k_b = mm_call([xb], [(wcat, 0, _wspec(k, tn, l, OFF_K // tn))], [(c128, r128), (s128, r128)],
                       [(_sds((m, 2048), F32), _ospec(tm, tn)), (_sds((m, 2048), BF), _ospec(tm, tn))],
                       _epi_rope(HEAD_DIM // 2), tm=tm, ncol=nb)
    v_f, v_b = mm_call([xb], [(wcat, 0, _wspec(k, tn, l, OFF_V // tn))], [],
                       [(_sds((m, 2048), F32), _ospec(tm, tn)), (_sds((m, 2048), BF), _ospec(tm, tn))],
                       _epi_plain, tm=tm, ncol=nb)
    iq_b, = mm_call([xb], [(wcat, 0, _wspec(k, tn, l, OFF_IQ // tn))], [(c64, r128), (s64, r128)],
                    [(_sds((m, 2048), BF), _ospec(tm, tn))], _epi_rope(IDX_DIM // 2), tm=tm, ncol=nb)
    ik_f, ik_b, iw = mm_call(
        [xb], [(wcat, 0, _wspec(k, LANES, l, OFF_IKW // LANES))], [(c64, r128), (s64, r128)],
        [(_sds((m, IDX_DIM), F32), _rowspec(tm, IDX_DIM)), (_sds((m, IDX_DIM), BF), _rowspec(tm, IDX_DIM)),
         (_sds((m, IDX_HEADS), F32), _rowspec(tm, IDX_HEADS))], _epi_ikw, tm=tm, ncol=1)
    u, = mm_call([xb], [(wcat, 0, _wspec(k, tn, l, OFF_GA // tn)), (wcat, 0, _wspec(k, tn, l, OFF_GG // tn))], [],
                 [(_sds((m, D_CONV), F32), _ospec(tm, tn))], _epi_glu, tm=tm, ncol=D_CONV // tn)
    return q_b, k_f, k_b, v_f, v_b, iq_b, ik_f, ik_b, iw, u


def merge_call(xb, attn_b, conv_b, l, wcat, wao, wco, tm):
    m = xb.shape[0]
    tn = 256
    merged, = mm_call(
        [xb, attn_b, conv_b],
        [(wcat, 0, _wspec(D_MODEL, tn, l, OFF_GTA // tn)), (wcat, 0, _wspec(D_MODEL, tn, l, OFF_GTB // tn)),
         (wao, 1, _wspec(ATTN_WIDTH, tn, l, 0)), (wco, 2, _wspec(D_CONV, tn, l, 0))],
        [], [(_sds((m, D_MODEL), BF), _ospec(tm, tn))], _epi_merge, tm=tm, ncol=D_MODEL // tn)
    return merged


def moe_up_call(xb, gate, l, wg, wu, tm):
    m = xb.shape[0]
    tn = 512
    nb = D_EXPERT // tn
    espec = pl.BlockSpec((None, None, D_MODEL, tn), lambda i, j: (l, j // nb, 0, j % nb))
    h, = mm_call([xb], [(wg, 0, espec), (wu, 0, espec)], [(gate, _rowspec(tm, LANES))],
                 [(_sds((m, N_EXPERTS * D_EXPERT), BF), _ospec(tm, tn))], _epi_moe_up(nb),
                 tm=tm, ncol=N_EXPERTS * nb)
    return h


def _rowmm_ln_kernel(a_ref, w_ref, x_ref, g_ref, b_ref, of_ref, ob_ref, acc_ref, *, nk):
    kk = pl.program_id(1)

    @pl.when(kk == 0)
    def _():
        acc_ref[...] = jnp.zeros_like(acc_ref)

    acc_ref[...] += jnp.dot(a_ref[...], w_ref[...], preferred_element_type=F32)

    @pl.when(kk == nk - 1)
    def _():
        y = _layer_norm(DN_ALPHA * x_ref[...] + acc_ref[...], g_ref[...], b_ref[...])
        of_ref[...] = y
        ob_ref[...] = y.astype(BF)


def rowmm_ln_call(a, w, l, x, g, b, tm, tk):
    m, k = a.shape
    nk = k // tk
    vec = pl.BlockSpec((None, 1, D_MODEL), lambda i, kk: (l, 0, 0))
    row = pl.BlockSpec((tm, D_MODEL), lambda i, kk: (i, 0))
    return pl.pallas_call(
        functools.partial(_rowmm_ln_kernel, nk=nk), grid=(m // tm, nk),
        in_specs=[pl.BlockSpec((tm, tk), lambda i, kk: (i, kk)),
                  pl.BlockSpec((None, tk, D_MODEL), lambda i, kk: (l, kk, 0)), row, vec, vec],
        out_specs=[row, row], out_shape=[_sds((m, D_MODEL), F32), _sds((m, D_MODEL), BF)],
        scratch_shapes=[pltpu.VMEM((tm, D_MODEL), F32)],
        compiler_params=_cp("parallel", "arbitrary"),
    )(a, w, x, g, b)


def _router_kernel(x_ref, wr_ref, br_ref, gate_ref):
    tm = x_ref.shape[0]
    logits = lax.dot_general(wr_ref[...], x_ref[...], NT, preferred_element_type=F32)
    e = jnp.exp(logits - jnp.max(logits, axis=0, keepdims=True))
    probs = e / jnp.sum(e, axis=0, keepdims=True)
    sel = probs + br_ref[...]
    srow = [sel[i:i + 1, :] for i in range(N_EXPERTS)]
    prow = [probs[i:i + 1, :] for i in range(N_EXPERTS)]
    per = N_EXPERTS // N_GROUPS
    score = []
    for g in range(N_GROUPS):
        a, b, c, d = srow[per * g:per * g + per]
        hi1, lo1, hi2, lo2 = jnp.maximum(a, b), jnp.minimum(a, b), jnp.maximum(c, d), jnp.minimum(c, d)
        score.append(jnp.maximum(hi1, hi2) + jnp.maximum(jnp.minimum(hi1, hi2), jnp.maximum(lo1, lo2)))
    best = jnp.zeros((1, tm), jnp.int32)
    best_v = score[0]
    for g in range(1, N_GROUPS):
        up = score[g] > best_v
        best = jnp.where(up, g, best)
        best_v = jnp.where(up, score[g], best_v)
    chosen = []
    for i in range(N_EXPERTS):
        g = i // per
        rank = jnp.zeros((1, tm), jnp.int32)
        for j in range(per * g, per * g + per):
            if j == i:
                continue
            ahead = (srow[j] > srow[i]) | ((srow[j] == srow[i]) & (j < i))
            rank = rank + ahead.astype(jnp.int32)
        chosen.append((best == g) & (rank < 2))
    den = jnp.zeros((1, tm), F32)
    for i in range(N_EXPERTS):
        den = den + jnp.where(chosen[i], prow[i], 0.0)
    rows = [jnp.where(chosen[i], prow[i] / den, 0.0) for i in range(N_EXPERTS)]
    gt = jnp.concatenate(rows + [jnp.zeros((LANES - N_EXPERTS, tm), F32)], axis=0)
    gate_ref[...] = gt.T


def router_call(xb, wr_t, br, tm):
    m = xb.shape[0]
    return pl.pallas_call(
        _router_kernel, grid=(m // tm,),
        in_specs=[pl.BlockSpec((tm, D_MODEL), lambda i: (i, 0)),
                  pl.BlockSpec((N_EXPERTS, D_MODEL), lambda i: (0, 0)),
                  pl.BlockSpec((N_EXPERTS, 1), lambda i: (0, 0))],
        out_specs=pl.BlockSpec((tm, LANES), lambda i: (i, 0)), out_shape=_sds((m, LANES), F32),
        compiler_params=_cp("parallel"),
    )(xb, wr_t, br)


def _order_key(score):
    bits = pltpu.bitcast(score, jnp.int32)
    return jnp.where(bits < 0, (bits ^ 0x7FFFFFFF) + 1, bits)


def _kth_largest_key(count_ge, shape):
    def body(i, ans):
        cand = ans + jnp.left_shift(jnp.int32(1), 31 - i)
        return jnp.where(count_ge(cand) >= N_SEL, cand, ans)
    return lax.fori_loop(0, 32, body, jnp.full(shape, INT_MIN, jnp.int32))


def _pidx_kernel(iq_ref, iw_ref, ik_ref, bias_ref, sc_ref, key_ref, *, tq, t_len):
    qi = pl.program_id(1)
    ik = ik_ref[...]
    w = iw_ref[...] * (IDX_HEADS ** -0.5)
    for h in range(IDX_HEADS):
        d = lax.dot_general(iq_ref[:, h * IDX_DIM:(h + 1) * IDX_DIM], ik, NT, preferred_element_type=F32)
        t = jnp.maximum(d * (IDX_DIM ** -0.5), 0.0) * w[:, h:h + 1]
        if h == 0:
            sc_ref[...] = t
        else:
            sc_ref[...] += t
    row = qi * tq + lax.broadcasted_iota(jnp.int32, (tq, t_len), 0)
    col = lax.broadcasted_iota(jnp.int32, (tq, t_len), 1)
    key_ref[...] = jnp.where(col <= row, _order_key(sc_ref[...]), INT_MIN)

    def count_ge(cand):
        return jnp.sum((key_ref[...] >= cand).astype(jnp.int32), axis=1, keepdims=True)

    thr = _kth_largest_key(count_ge, (tq, 1))
    key = key_ref[...]
    gt = key > thr
    eq = key == thr
    need = N_SEL - jnp.sum(gt.astype(jnp.int32), axis=1, keepdims=True)
    n_eq = jnp.sum(eq.astype(jnp.int32), axis=1, keepdims=True)
    causal = key > INT_MIN
    bias_ref[...] = jnp.where((gt | eq) & causal, 0.0, NEG).astype(BF)
    tie = ((n_eq > need) & (thr > INT_MIN)).astype(jnp.int32)

    @pl.when(jnp.max(tie) > 0)
    def _():
        cw = 256
        upper = (lax.broadcasted_iota(jnp.int32, (cw, cw), 0)
                 < lax.broadcasted_iota(jnp.int32, (cw, cw), 1)).astype(BF)
        run = jnp.zeros((tq, 1), F32)
        need_f = need.astype(F32)
        for c in range(t_len // cw):
            kc = key_ref[:, c * cw:(c + 1) * cw]
            eq_c = kc == thr
            eq_f = jnp.where(eq_c, 1.0, 0.0)
            rank = run + jnp.dot(eq_f.astype(BF), upper, preferred_element_type=F32)
            keep = ((kc > thr) | (eq_c & (rank < need_f))) & (kc > INT_MIN)
            bias_ref[:, c * cw:(c + 1) * cw] = jnp.where(keep, 0.0, NEG).astype(BF)
            run = run + jnp.sum(eq_f, axis=1, keepdims=True)


def prompt_index_call(iq_b, iw, ik_b, tq):
    nq = SEQ // tq
    return pl.pallas_call(
        functools.partial(_pidx_kernel, tq=tq, t_len=SEQ), grid=(BATCH, nq),
        in_specs=[pl.BlockSpec((tq, IDX_HEADS * IDX_DIM), lambda b, q: (b * nq + q, 0)),
                  pl.BlockSpec((tq, IDX_HEADS), lambda b, q: (b * nq + q, 0)),
                  pl.BlockSpec((SEQ, IDX_DIM), lambda b, q: (b, 0))],
        out_specs=pl.BlockSpec((tq, SEQ), lambda b, q: (b * nq + q, 0)),
        out_shape=_sds((M_PROMPT, SEQ), BF),
        scratch_shapes=[pltpu.VMEM((tq, SEQ), F32), pltpu.VMEM((tq, SEQ), jnp.int32)],
        compiler_params=_cp("parallel", "arbitrary"),
    )(iq_b, iw, ik_b)


def _pattn_kernel(q_ref, k_ref, v_ref, bias_ref, o_ref):
    s = lax.dot_general(q_ref[...], k_ref[...], NT, preferred_element_type=F32) * (HEAD_DIM ** -0.5)
    s = s + bias_ref[...].astype(F32)
    p = jnp.exp(s - jnp.max(s, axis=1, keepdims=True))
    o = jnp.dot(p.astype(BF), v_ref[...], preferred_element_type=F32)
    o_ref[...] = (o / jnp.sum(p, axis=1, keepdims=True)).astype(BF)


def prompt_attn_call(q_b, k_b, v_b, bias, tq):
    nq = SEQ // tq
    qspec = pl.BlockSpec((tq, HEAD_DIM), lambda b, h, q: (b * nq + q, h))
    kspec = pl.BlockSpec((SEQ, HEAD_DIM), lambda b, h, q: (b, h))
    return pl.pallas_call(
        _pattn_kernel, grid=(BATCH, N_HEADS, nq),
        in_specs=[qspec, kspec, kspec, pl.BlockSpec((tq, SEQ), lambda b, h, q: (b * nq + q, 0))],
        out_specs=qspec, out_shape=_sds((M_PROMPT, ATTN_WIDTH), BF),
        compiler_params=_cp("parallel", "parallel", "arbitrary"),
    )(q_b, k_b, v_b, bias)


HALO = 32


def _pconv_kernel(u_ref, halo_ref, w_ref, bdw_ref, g_ref, b_ref, o_ref, ext_ref, y_ref, *, tt):
    t = pl.program_id(1)
    ext_ref[0:HALO, :] = jnp.where(t == 0, 0.0, halo_ref[...])
    ext_ref[HALO:HALO + tt, :] = u_ref[...]
    rb_n, cb_n = 64, 256
    first = HALO - (CONV_WIDTH - 1)
    for cc in range(D_CONV // cb_n):
        cs = slice(cc * cb_n, (cc + 1) * cb_n)
        for rb in range(tt // rb_n):
            acc = jnp.zeros((rb_n, cb_n), F32)
            for j in range(CONV_WIDTH):
                acc = acc + ext_ref[rb * rb_n + first + j:rb * rb_n + first + j + rb_n, cs] * w_ref[j:j + 1, cs]
            y_ref[rb * rb_n:(rb + 1) * rb_n, cs] = acc + bdw_ref[:, cs]
    y = _layer_norm(y_ref[...], g_ref[...], b_ref[...])
    o_ref[...] = (y * _sigmoid(y)).astype(BF)


def prompt_conv_call(u, l, w_dw, b_dw, g, b, tt):
    nt = SEQ // tt
    per = tt // HALO
    vec = pl.BlockSpec((None, 1, D_CONV), lambda bb, t: (l, 0, 0))
    return pl.pallas_call(
        functools.partial(_pconv_kernel, tt=tt), grid=(BATCH, nt),
        in_specs=[pl.BlockSpec((tt, D_CONV), lambda bb, t: (bb * nt + t, 0)),
                  pl.BlockSpec((HALO, D_CONV), lambda bb, t: (jnp.maximum((bb * nt + t) * per - 1, 0), 0)),
                  pl.BlockSpec((None, CONV_WIDTH, D_CONV), lambda bb, t: (l, 0, 0)), vec, vec, vec],
        out_specs=pl.BlockSpec((tt, D_CONV), lambda bb, t: (bb * nt + t, 0)),
        out_shape=_sds((M_PROMPT, D_CONV), BF),
        scratch_shapes=[pltpu.VMEM((HALO + tt, D_CONV), F32), pltpu.VMEM((tt, D_CONV), F32)],
        compiler_params=_cp("parallel", "arbitrary"),
    )(u, u, w_dw, b_dw, g, b)


def _sconv_kernel(u_ref, st_ref, w_ref, bdw_ref, g_ref, b_ref, o_ref):
    w = w_ref[...]
    y = jnp.sum(st_ref[...] * w[None, :CONV_WIDTH - 1, :], axis=1)
    y = y + u_ref[0:DEC_BATCH, :] * w[CONV_WIDTH - 1:CONV_WIDTH, :] + bdw_ref[...]
    y = _layer_norm(y, g_ref[...], b_ref[...])
    y = y * _sigmoid(y)
    o_ref[...] = jnp.concatenate([y, jnp.zeros((M_SAMPLE - DEC_BATCH, D_CONV), F32)], axis=0).astype(BF)


def sample_conv_call(u, state, l, w_dw, b_dw, g, b):
    vec = pl.BlockSpec((None, 1, D_CONV), lambda i: (l, 0, 0))
    return pl.pallas_call(
        _sconv_kernel, grid=(1,),
        in_specs=[pl.BlockSpec((M_SAMPLE, D_CONV), lambda i: (0, 0)),
                  pl.BlockSpec((None, DEC_BATCH, CONV_WIDTH - 1, D_CONV), lambda i: (l, 0, 0, 0)),
                  pl.BlockSpec((None, CONV_WIDTH, D_CONV), lambda i: (l, 0, 0)), vec, vec, vec],
        out_specs=pl.BlockSpec((M_SAMPLE, D_CONV), lambda i: (0, 0)),
        out_shape=_sds((M_SAMPLE, D_CONV), BF),
        compiler_params=_cp("arbitrary"),
    )(u, state, w_dw, b_dw, g, b)


def _sidx_kernel(pt_ref, iq_ref, iw_ref, ikn_ref, page_ref, mp_ref, mn_ref, sc_ref):
    p = pl.program_id(1)
    w = iw_ref[...] * (IDX_HEADS ** -0.5)
    iq = iq_ref[...]
    d = lax.dot_general(iq, page_ref[...].astype(BF), NT, preferred_element_type=F32)
    sc_ref[pl.ds(p, 1), :] = jnp.sum(jnp.maximum(d * (IDX_DIM ** -0.5), 0.0) * w, axis=0, keepdims=True)

    @pl.when(p == N_PAGES - 1)
    def _():
        dn = jnp.sum(iq.astype(F32) * ikn_ref[...].astype(F32), axis=1, keepdims=True)
        sn = jnp.sum(jnp.maximum(dn * (IDX_DIM ** -0.5), 0.0) * w, axis=0, keepdims=True)
        kp = _order_key(sc_ref[...])
        kn = _order_key(sn)

        def total(mask):
            return jnp.sum(jnp.sum(mask.astype(jnp.int32), axis=0, keepdims=True), axis=1, keepdims=True)

        thr = _kth_largest_key(lambda c: total(kp >= c) + (kn >= c).astype(jnp.int32), (1, 1))
        gt, eq = kp > thr, kp == thr
        need = (N_SEL - total(gt) - (kn > thr).astype(jnp.int32)).astype(F32)
        eq_f = jnp.where(eq, 1.0, 0.0)
        upper = (lax.broadcasted_iota(jnp.int32, (PAGE_SIZE, PAGE_SIZE), 0)
                 < lax.broadcasted_iota(jnp.int32, (PAGE_SIZE, PAGE_SIZE), 1)).astype(BF)
        lower = (lax.broadcasted_iota(jnp.int32, (N_PAGES, N_PAGES), 1)
                 < lax.broadcasted_iota(jnp.int32, (N_PAGES, N_PAGES), 0)).astype(BF)
        in_page = jnp.dot(eq_f.astype(BF), upper, preferred_element_type=F32)
        page_tot = jnp.broadcast_to(jnp.sum(eq_f, axis=1, keepdims=True), (N_PAGES, PAGE_SIZE))
        before = jnp.dot(lower, page_tot.astype(BF), preferred_element_type=F32)
        keep = gt | (eq & (in_page + before < need))
        mp_ref[...] = jnp.where(keep, 0.0, NEG)
        n_eq = total(eq).astype(F32)
        keep_n = (kn > thr) | ((kn == thr) & (n_eq < need))
        mn_ref[...] = jnp.broadcast_to(jnp.where(keep_n, 0.0, NEG), (1, LANES))


def sample_index_call(page_table, iq3, iw3, ikn3, cache_idx_k, l):
    grid_spec = pltpu.PrefetchScalarGridSpec(
        num_scalar_prefetch=1, grid=(DEC_BATCH, N_PAGES),
        in_specs=[pl.BlockSpec((None, IDX_HEADS, IDX_DIM), lambda b, p, pt: (b, 0, 0)),
                  pl.BlockSpec((None, IDX_HEADS, 1), lambda b, p, pt: (b, 0, 0)),
                  pl.BlockSpec((None, 1, IDX_DIM), lambda b, p, pt: (b, 0, 0)),
                  pl.BlockSpec((None, None, PAGE_SIZE, IDX_DIM), lambda b, p, pt: (l, pt[b, p], 0, 0))],
        out_specs=[pl.BlockSpec((None, N_PAGES, PAGE_SIZE), lambda b, p, pt: (b, 0, 0)),
                   pl.BlockSpec((None, 1, LANES), lambda b, p, pt: (b, 0, 0))],
        scratch_shapes=[pltpu.VMEM((N_PAGES, PAGE_SIZE), F32)])
    return pl.pallas_call(
        _sidx_kernel, grid_spec=grid_spec,
        out_shape=[_sds((DEC_BATCH, N_PAGES, PAGE_SIZE), F32), _sds((DEC_BATCH, 1, LANES), F32)],
        compiler_params=_cp("parallel", "arbitrary"),
    )(page_table, iq3, iw3, ikn3, cache_idx_k)


def _sattn_kernel(pt_ref, q_ref, kn_ref, vn_ref, mp_ref, mn_ref, kp_ref, vp_ref, o_ref, m_ref, l_ref, acc_ref):
    p = pl.program_id(1)
    scale = HEAD_DIM ** -0.5

    @pl.when(p == 0)
    def _():
        m_ref[...] = jnp.full_like(m_ref, -3e38)
        l_ref[...] = jnp.zeros_like(l_ref)
        acc_ref[...] = jnp.zeros_like(acc_ref)

    own = (lax.broadcasted_iota(jnp.int32, (N_HEADS, ATTN_WIDTH), 0)
           == lax.broadcasted_iota(jnp.int32, (N_HEADS, ATTN_WIDTH), 1) // HEAD_DIM)
    q_heads = jnp.where(own, jnp.broadcast_to(q_ref[...].astype(F32), (N_HEADS, ATTN_WIDTH)), 0.0)
    s = lax.dot_general(q_heads.astype(BF), kp_ref[...].astype(BF), NT, preferred_element_type=F32) * scale
    s = s + mp_ref[pl.ds(p, 1), :]
    m_new = jnp.maximum(m_ref[...], jnp.max(s, axis=1, keepdims=True))
    alpha = jnp.exp(m_ref[...] - m_new)
    pr = jnp.exp(s - m_new)
    l_ref[...] = alpha * l_ref[...] + jnp.sum(pr, axis=1, keepdims=True)
    acc_ref[...] = alpha * acc_ref[...] + jnp.dot(pr.astype(BF), vp_ref[...].astype(BF), preferred_element_type=F32)
    m_ref[...] = m_new

    @pl.when(p == N_PAGES - 1)
    def _():
        sn = jnp.sum(q_heads * kn_ref[...].astype(F32), axis=1, keepdims=True) * scale + mn_ref[:, 0:1]
        m_fin = jnp.maximum(m_ref[...], sn)
        a_fin = jnp.exp(m_ref[...] - m_fin)
        pn = jnp.exp(sn - m_fin)
        l_fin = a_fin * l_ref[...] + pn
        acc = a_fin * acc_ref[...] + pn.astype(BF).astype(F32) * vn_ref[...].astype(F32)
        o = jnp.sum(jnp.where(own, acc / l_fin, 0.0), axis=0, keepdims=True)
        o_ref[...] = o.astype(BF)


def sample_attn_call(page_table, q3, kn3, vn3, mp, mn, cache_k, cache_v, l):
    row = pl.BlockSpec((None, 1, ATTN_WIDTH), lambda b, p, pt: (b, 0, 0))
    page = pl.BlockSpec((None, None, PAGE_SIZE, ATTN_WIDTH), lambda b, p, pt: (l, pt[b, p], 0, 0))
    grid_spec = pltpu.PrefetchScalarGridSpec(
        num_scalar_prefetch=1, grid=(DEC_BATCH, N_PAGES),
        in_specs=[row, row, row,
                  pl.BlockSpec((None, N_PAGES, PAGE_SIZE), lambda b, p, pt: (b, 0, 0)),
                  pl.BlockSpec((None, 1, LANES), lambda b, p, pt: (b, 0, 0)), page, page],
        out_specs=row,
        scratch_shapes=[pltpu.VMEM((N_HEADS, 1), F32), pltpu.VMEM((N_HEADS, 1), F32),
                        pltpu.VMEM((N_HEADS, ATTN_WIDTH), F32)])
    return pl.pallas_call(
        _sattn_kernel, grid_spec=grid_spec, out_shape=_sds((DEC_BATCH, 1, ATTN_WIDTH), BF),
        compiler_params=_cp("parallel", "arbitrary"),
    )(page_table, q3, kn3, vn3, mp, mn, cache_k, cache_v)


def _rope_tables(pos):
    def tab(d):
        inv = ROPE_THETA ** (-jnp.arange(0, d, 2, dtype=F32) / d)
        ang = pos.astype(F32)[:, None] * inv[None, :]
        cos, sin = jnp.cos(ang), jnp.sin(ang)
        reps = LANES // d
        return (jnp.tile(jnp.concatenate([cos, cos], axis=1), (1, reps)),
                jnp.tile(jnp.concatenate([-sin, sin], axis=1), (1, reps)))
    c128, s128 = tab(HEAD_DIM)
    c64, s64 = tab(IDX_DIM)
    return c128, s128, c64, s64


def _finish(xf, xb, attn_b, conv_b, l, wts, tm, tm_row):
    wcat, wao, wco, wout, ln1g, ln1b, ln2g, ln2b, wr_t, br, wg, wu, wd = wts
    merged = merge_call(xb, attn_b, conv_b, l, wcat, wao, wco, tm)
    x1f, x1b = rowmm_ln_call(merged, wout, l, xf, ln1g, ln1b, tm_row, 512)
    gate = router_call(x1b, wr_t, br, tm_row)
    h = moe_up_call(x1b, gate, l, wg, wu, tm)
    return rowmm_ln_call(h, wd, l, x1f, ln2g, ln2b, tm_row, 512)


def kernel(x_prompt, x_sample, cache_k, cache_v, cache_idx_k, state_conv, page_table, w_in, w_dw, b_dw,
           conv_ln_g, conv_ln_b, w_attn_out, w_conv_out, w_out, ln1_g, ln1_b, ln2_g, ln2_b, w_router, b_router,
           w_gate, w_up, w_down):
    n_pool = cache_k.shape[1]
    wcat = jnp.concatenate(
        [w_in[:, :, :OFF_GA], w_in[:, :, OFF_GA + 96:], w_in[:, :, OFF_GA:OFF_GA + 96],
         jnp.zeros((DEPTH, D_MODEL, W_CAT - OFF_IKW - 96), F32)], axis=2).astype(BF)
    wts = (wcat, w_attn_out.astype(BF), w_conv_out.astype(BF), w_out.astype(BF),
           ln1_g[:, None, :], ln1_b[:, None, :], ln2_g[:, None, :], ln2_b[:, None, :],
           w_router.T.astype(BF), b_router[:, None], w_gate.astype(BF), w_up.astype(BF),
           w_down.astype(BF).reshape(DEPTH, N_EXPERTS * D_EXPERT, D_MODEL))
    b_dw3, cg3, cb3 = b_dw[:, None, :], conv_ln_g[:, None, :], conv_ln_b[:, None, :]
    ck = cache_k.reshape(DEPTH, n_pool, PAGE_SIZE, ATTN_WIDTH)
    cv = cache_v.reshape(DEPTH, n_pool, PAGE_SIZE, ATTN_WIDTH)

    tabs_p = _rope_tables(jnp.tile(jnp.arange(SEQ), BATCH))
    tabs_s = _rope_tables(jnp.full((M_SAMPLE,), PAST_LEN))

    xpf = x_prompt.reshape(M_PROMPT, D_MODEL)
    xsf = jnp.concatenate([x_sample.reshape(DEC_BATCH, D_MODEL),
                           jnp.zeros((M_SAMPLE - DEC_BATCH, D_MODEL), F32)], axis=0)
    xpb, xsb = xpf.astype(BF), xsf.astype(BF)

    outs = [[] for _ in range(8)]
    for l in range(DEPTH):
        q_b, k_f, k_b, v_f, v_b, iq_b, ik_f, ik_b, iw, u = in_proj(xpb, l, wcat, tabs_p, 1024)
        bias = prompt_index_call(iq_b, iw, ik_b, 256)
        attn_b = prompt_attn_call(q_b, k_b, v_b, bias, 256)
        conv_b = prompt_conv_call(u, l, w_dw, b_dw3, cg3, cb3, 128)
        xpf, xpb = _finish(xpf, xpb, attn_b, conv_b, l, wts, 1024, 256)
        outs[0].append(k_f.reshape(M_PROMPT // PAGE_SIZE, PAGE_SIZE, N_HEADS, HEAD_DIM))
        outs[1].append(v_f.reshape(M_PROMPT // PAGE_SIZE, PAGE_SIZE, N_HEADS, HEAD_DIM))
        outs[2].append(ik_f.reshape(M_PROMPT // PAGE_SIZE, PAGE_SIZE, IDX_DIM))
        outs[3].append(u.reshape(BATCH, SEQ, D_CONV)[:, SEQ - (CONV_WIDTH - 1):, :])

        q_b, k_f, k_b, v_f, v_b, iq_b, ik_f, ik_b, iw, u = in_proj(xsb, l, wcat, tabs_s, M_SAMPLE)
        mp, mn = sample_index_call(
            page_table, iq_b[:DEC_BATCH].reshape(DEC_BATCH, IDX_HEADS, IDX_DIM),
            iw[:DEC_BATCH].reshape(DEC_BATCH, IDX_HEADS, 1), ik_b[:DEC_BATCH].reshape(DEC_BATCH, 1, IDX_DIM),
            cache_idx_k, l)
        attn_s = sample_attn_call(
            page_table, q_b[:DEC_BATCH].reshape(DEC_BATCH, 1, ATTN_WIDTH),
            k_b[:DEC_BATCH].reshape(DEC_BATCH, 1, ATTN_WIDTH), v_b[:DEC_BATCH].reshape(DEC_BATCH, 1, ATTN_WIDTH),
            mp, mn, ck, cv, l)
        attn_b = jnp.concatenate([attn_s.reshape(DEC_BATCH, ATTN_WIDTH),
                                  jnp.zeros((M_SAMPLE - DEC_BATCH, ATTN_WIDTH), BF)], axis=0)
        conv_b = sample_conv_call(u, state_conv, l, w_dw, b_dw3, cg3, cb3)
        xsf, xsb = _finish(xsf, xsb, attn_b, conv_b, l, wts, M_SAMPLE, M_SAMPLE)
        outs[4].append(k_f[:DEC_BATCH].reshape(DEC_BATCH, 1, N_HEADS, HEAD_DIM))
        outs[5].append(v_f[:DEC_BATCH].reshape(DEC_BATCH, 1, N_HEADS, HEAD_DIM))
        outs[6].append(ik_f[:DEC_BATCH].reshape(DEC_BATCH, 1, IDX_DIM))
        outs[7].append(jnp.concatenate([state_conv[l][:, 1:, :], u[:DEC_BATCH, None, :]], axis=1))

    y_prompt = xpf.reshape(BATCH, SEQ, D_MODEL)
    y_sample = xsf[:DEC_BATCH].reshape(DEC_BATCH, 1, D_MODEL)
    return (y_prompt, y_sample) + tuple(jnp.stack(o) for o in outs)
```

```python
import functools

import jax
import jax.numpy as jnp
from jax import lax
from jax.experimental import pallas as pl
from jax.experimental.pallas import tpu as pltpu

D_MODEL = 4096
BATCH = 4
SEQ = 2048
DEPTH = 4
DEC_BATCH = 8
PAST_LEN = 8192
PAGE_SIZE = 128
N_PAGES = PAST_LEN // PAGE_SIZE
N_HEADS = 16
HEAD_DIM = 128
ATTN_WIDTH = N_HEADS * HEAD_DIM
IDX_HEADS = 32
IDX_DIM = 64
N_SEL = 256
D_CONV = 2048
CONV_WIDTH = 31
N_EXPERTS = 16
N_GROUPS = 4
PER_GROUP = N_EXPERTS // N_GROUPS
D_EXPERT = 1024
ROPE_THETA = 10000.0
LN_EPS = 1e-5
DN_ALPHA = (2 * DEPTH) ** 0.25

F32 = jnp.float32
BF = jnp.bfloat16
I32 = jnp.int32
LANES = 128
SUBLANES = 8
VMEM_LIMIT = 56 * 1024 * 1024
NEG = -1e30
INT_MIN = -2 ** 31
NT = (((1,), (1,)), ((), ()))

OFF_Q, OFF_K, OFF_V, OFF_IQ, OFF_GA, OFF_GG, OFF_GTA, OFF_GTB, OFF_IKW = (
    0, 2048, 4096, 6144, 8192, 10240, 12288, 16384, 20480)
W_CAT = 20608

M_PROMPT = BATCH * SEQ
M_SAMPLE = 128
XG_WIDTH = D_MODEL + LANES
GROUP_LANE = N_EXPERTS
MOE_TILE = 256
MOE_ROWS = M_PROMPT + N_GROUPS * MOE_TILE
DMA_RING = 64
PAGES_PER_STEP = 8


def _cp(*sem):
    return pltpu.CompilerParams(dimension_semantics=sem, vmem_limit_bytes=VMEM_LIMIT)


def _sds(shape, dtype):
    return jax.ShapeDtypeStruct(shape, dtype)


def _sigmoid(x):
    return 1.0 / (1.0 + jnp.exp(-x))


def _bf_round(x):
    return x.astype(BF).astype(F32)


def _layer_norm(z, g, b):
    mu = jnp.mean(z, axis=-1, keepdims=True)
    zc = z - mu
    var = jnp.mean(zc * zc, axis=-1, keepdims=True)
    return zc * lax.rsqrt(var + LN_EPS) * g + b


def mm_call(xs, ws, extras, outs, epilogue, *, tm, ncol):
    m = xs[0].shape[0]
    nx, nw, ne = len(xs), len(ws), len(extras)
    xidx = [xi for (_, xi, _) in ws]

    def body(*refs):
        x_refs, w_refs = refs[:nx], refs[nx:nx + nw]
        e_refs, o_refs = refs[nx + nw:nx + nw + ne], refs[nx + nw + ne:]
        dots = [jnp.dot(x_refs[xi][...], w[...], preferred_element_type=F32) for xi, w in zip(xidx, w_refs)]
        epilogue(dots, e_refs, o_refs)

    in_specs = [pl.BlockSpec((tm, x.shape[1]), lambda i, j: (i, 0)) for x in xs]
    in_specs += [s for (_, _, s) in ws] + [s for (_, s) in extras]
    res = pl.pallas_call(
        body, grid=(m // tm, ncol), in_specs=in_specs,
        out_specs=[s for (_, s) in outs], out_shape=[sd for (sd, _) in outs],
        compiler_params=_cp("parallel", "arbitrary"),
    )(*xs, *[w for (w, _, _) in ws], *[e for (e, _) in extras])
    return res


def _wspec(k, tn, l, off_blocks):
    return pl.BlockSpec((None, k, tn), lambda i, j: (l, 0, off_blocks + j))


def _ospec(tm, tn):
    return pl.BlockSpec((tm, tn), lambda i, j: (i, j))


def _rowspec(tm, n, col_block=0):
    return pl.BlockSpec((tm, n), lambda i, j: (i, col_block))


def _rot_half(x, half):
    if 2 * half == LANES:
        return pltpu.roll(x, half, 1)
    lane = lax.broadcasted_iota(I32, x.shape, 1)
    return jnp.where((lane % (2 * half)) < half, pltpu.roll(x, LANES - half, 1), pltpu.roll(x, half, 1))


def _epi_rope(half):
    def epi(dots, e_refs, o_refs):
        d, c, s = dots[0], e_refs[0][...], e_refs[1][...]
        for ci in range(d.shape[1] // LANES):
            x = d[:, ci * LANES:(ci + 1) * LANES]
            y = x * c + _rot_half(x, half) * s
            for o in o_refs:
                o[:, ci * LANES:(ci + 1) * LANES] = y.astype(o.dtype)
    return epi


def _epi_plain(dots, e_refs, o_refs):
    for o in o_refs:
        o[...] = dots[0].astype(o.dtype)


def _epi_ikw(dots, e_refs, o_refs):
    d, c, s = dots[0], e_refs[0][...], e_refs[1][...]
    y = d * c + _rot_half(d, IDX_DIM // 2) * s
    o_refs[0][...] = y[:, :IDX_DIM]
    o_refs[1][...] = y[:, :IDX_DIM].astype(BF)
    o_refs[2][...] = d[:, IDX_DIM:IDX_DIM + IDX_HEADS]


def _epi_glu(dots, e_refs, o_refs):
    o_refs[0][...] = dots[0] * _sigmoid(dots[1])


def _epi_merge(dots, e_refs, o_refs):
    o_refs[0][...] = (_sigmoid(dots[0]) * dots[2] + _sigmoid(dots[1]) * dots[3]).astype(BF)


def _gate_column(gate, expert):
    lane = lax.broadcasted_iota(I32, gate.shape, 1)
    return jnp.sum(jnp.where(lane == expert, gate, 0.0), axis=1, keepdims=True)


def _epi_moe_up(nb):
    def epi(dots, e_refs, o_refs):
        gcol = _gate_column(e_refs[0][...], pl.program_id(1) // nb)
        a = dots[0]
        o_refs[0][...] = ((a * _sigmoid(a)) * dots[1] * gcol).astype(BF)
    return epi


def in_proj(xb, l, wcat, tabs, tm):
    m = xb.shape[0]
    c128, s128, c64, s64 = tabs
    tn = 512
    nb = 2048 // tn
    k = D_MODEL
    r128 = _rowspec(tm, LANES)
    q_b, = mm_call([xb], [(wcat, 0, _wspec(k, tn, l, OFF_Q // tn))], [(c128, r128), (s128, r128)],
                   [(_sds((m, 2048), BF), _ospec(tm, tn))], _epi_rope(HEAD_DIM // 2), tm=tm, ncol=nb)
    k_f, k_b = mm_call([xb], [(wcat, 0, _wspec(k, tn, l, OFF_K // tn))], [(c128, r128), (s128, r128)],
                       [(_sds((m, 2048), F32), _ospec(tm, tn)), (_sds((m, 2048), BF), _ospec(tm, tn))],
                       _epi_rope(HEAD_DIM // 2), tm=tm, ncol=nb)
    v_f, v_b = mm_call([xb], [(wcat, 0, _wspec(k, tn, l, OFF_V // tn))], [],
                       [(_sds((m, 2048), F32), _ospec(tm, tn)), (_sds((m, 2048), BF), _ospec(tm, tn))],
                       _epi_plain, tm=tm, ncol=nb)
    iq_b, = mm_call([xb], [(wcat, 0, _wspec(k, tn, l, OFF_IQ // tn))], [(c64, r128), (s64, r128)],
                    [(_sds((m, 2048), BF), _ospec(tm, tn))], _epi_rope(IDX_DIM // 2), tm=tm, ncol=nb)
    ik_f, ik_b, iw = mm_call(
        [xb], [(wcat, 0, _wspec(k, LANES, l, OFF_IKW // LANES))], [(c64, r128), (s64, r128)],
        [(_sds((m, IDX_DIM), F32), _rowspec(tm, IDX_DIM)), (_sds((m, IDX_DIM), BF), _rowspec(tm, IDX_DIM)),
         (_sds((m, IDX_HEADS), F32), _rowspec(tm, IDX_HEADS))], _epi_ikw, tm=tm, ncol=1)
    u, = mm_call([xb], [(wcat, 0, _wspec(k, tn, l, OFF_GA // tn)), (wcat, 0, _wspec(k, tn, l, OFF_GG // tn))], [],
                 [(_sds((m, D_CONV), F32), _ospec(tm, tn))], _epi_glu, tm=tm, ncol=D_CONV // tn)
    return q_b, k_f, k_b, v_f, v_b, iq_b, ik_f, ik_b, iw, u


def merge_call(xb, attn_b, conv_b, l, wcat, wao, wco, tm):
    m = xb.shape[0]
    tn = 256
    merged, = mm_call(
        [xb, attn_b, conv_b],
        [(wcat, 0, _wspec(D_MODEL, tn, l, OFF_GTA // tn)), (wcat, 0, _wspec(D_MODEL, tn, l, OFF_GTB // tn)),
         (wao, 1, _wspec(ATTN_WIDTH, tn, l, 0)), (wco, 2, _wspec(D_CONV, tn, l, 0))],
        [], [(_sds((m, D_MODEL), BF), _ospec(tm, tn))], _epi_merge, tm=tm, ncol=D_MODEL // tn)
    return merged


def moe_up_call(xb, xg, l, wg, wu, tm):
    m = xb.shape[0]
    tn = 512
    nb = D_EXPERT // tn
    espec = pl.BlockSpec((None, None, D_MODEL, tn), lambda i, j: (l, j // nb, 0, j % nb))
    h, = mm_call([xb], [(wg, 0, espec), (wu, 0, espec)], [(xg, _rowspec(tm, LANES, D_MODEL // LANES))],
                 [(_sds((m, N_EXPERTS * D_EXPERT), BF), _ospec(tm, tn))], _epi_moe_up(nb),
                 tm=tm, ncol=N_EXPERTS * nb)
    return h


def _route(logits, b_router):
    tm = logits.shape[1]
    e = jnp.exp(logits - jnp.max(logits, axis=0, keepdims=True))
    probs = e / jnp.sum(e, axis=0, keepdims=True)
    sel = probs + b_router
    srow = [sel[i:i + 1, :] for i in range(N_EXPERTS)]
    prow = [probs[i:i + 1, :] for i in range(N_EXPERTS)]
    score = []
    for g in range(N_GROUPS):
        a, b, c, d = srow[PER_GROUP * g:PER_GROUP * (g + 1)]
        hi1, lo1, hi2, lo2 = jnp.maximum(a, b), jnp.minimum(a, b), jnp.maximum(c, d), jnp.minimum(c, d)
        score.append(jnp.maximum(hi1, hi2) + jnp.maximum(jnp.minimum(hi1, hi2), jnp.maximum(lo1, lo2)))
    best = jnp.zeros((1, tm), I32)
    best_v = score[0]
    for g in range(1, N_GROUPS):
        up = score[g] > best_v
        best = jnp.where(up, g, best)
        best_v = jnp.where(up, score[g], best_v)
    chosen = []
    for i in range(N_EXPERTS):
        g = i // PER_GROUP
        rank = jnp.zeros((1, tm), I32)
        for j in range(PER_GROUP * g, PER_GROUP * (g + 1)):
            if j == i:
                continue
            ahead = (srow[j] > srow[i]) | ((srow[j] == srow[i]) & (j < i))
            rank = rank + ahead.astype(I32)
        chosen.append((best == g) & (rank < 2))
    den = jnp.zeros((1, tm), F32)
    for i in range(N_EXPERTS):
        den = den + jnp.where(chosen[i], prow[i], 0.0)
    return [jnp.where(chosen[i], prow[i] / den, 0.0) for i in range(N_EXPERTS)], best


def _rowmm_ln_kernel(a_ref, w_ref, x_ref, g_ref, b_ref, of_ref, ob_ref, acc_ref, *, nk):
    kk = pl.program_id(1)

    @pl.when(kk == 0)
    def _():
        acc_ref[...] = jnp.zeros_like(acc_ref)

    acc_ref[...] += jnp.dot(a_ref[...], w_ref[...], preferred_element_type=F32)

    @pl.when(kk == nk - 1)
    def _():
        y = _layer_norm(DN_ALPHA * x_ref[...] + acc_ref[...], g_ref[...], b_ref[...])
        of_ref[...] = y
        ob_ref[...] = y.astype(BF)


def rowmm_ln_call(a, w, l, x, g, b, tm, tk):
    m, k = a.shape
    nk = k // tk
    vec = pl.BlockSpec((None, 1, D_MODEL), lambda i, kk: (l, 0, 0))
    row = pl.BlockSpec((tm, D_MODEL), lambda i, kk: (i, 0))
    return pl.pallas_call(
        functools.partial(_rowmm_ln_kernel, nk=nk), grid=(m // tm, nk),
        in_specs=[pl.BlockSpec((tm, tk), lambda i, kk: (i, kk)),
                  pl.BlockSpec((None, tk, D_MODEL), lambda i, kk: (l, kk, 0)), row, vec, vec],
        out_specs=[row, row], out_shape=[_sds((m, D_MODEL), F32), _sds((m, D_MODEL), BF)],
        scratch_shapes=[pltpu.VMEM((tm, D_MODEL), F32)],
        compiler_params=_cp("parallel", "arbitrary"),
    )(a, w, x, g, b)


def _rowmm_ln_route_kernel(a_ref, w_ref, x_ref, g_ref, b_ref, wr_ref, br_ref,
                           xg_ref, ob_ref, grp_ref, rank_ref, cnt_ref, acc_ref, run_ref, *, nk):
    i, kk = pl.program_id(0), pl.program_id(1)
    tm = x_ref.shape[0]

    @pl.when((i == 0) & (kk == 0))
    def _():
        run_ref[...] = jnp.zeros_like(run_ref)

    @pl.when(kk == 0)
    def _():
        acc_ref[...] = jnp.zeros_like(acc_ref)

    acc_ref[...] += jnp.dot(a_ref[...], w_ref[...], preferred_element_type=F32)

    @pl.when(kk == nk - 1)
    def _():
        y = _layer_norm(DN_ALPHA * x_ref[...] + acc_ref[...], g_ref[...], b_ref[...])
        yb = y.astype(BF)
        ob_ref[...] = yb
        logits = lax.dot_general(wr_ref[...], yb, NT, preferred_element_type=F32)
        gates, best = _route(logits, br_ref[...])
        pad = jnp.zeros((LANES - N_EXPERTS - 1, tm), F32)
        gt = jnp.concatenate(gates + [best.astype(F32), pad], axis=0)
        xg_ref[:, :D_MODEL] = y
        xg_ref[:, D_MODEL:] = gt.T
        one_hot = jnp.concatenate([jnp.where(best == g, 1.0, 0.0) for g in range(N_GROUPS)]
                                  + [jnp.zeros((SUBLANES - N_GROUPS, tm), F32)], axis=0)
        upper = (lax.broadcasted_iota(I32, (tm, tm), 0) < lax.broadcasted_iota(I32, (tm, tm), 1)).astype(BF)
        before = jnp.dot(one_hot.astype(BF), upper, preferred_element_type=F32) + run_ref[:, 0:1]
        rank_ref[...] = jnp.sum(one_hot * before, axis=0, keepdims=True).astype(I32)
        grp_ref[...] = best
        run = run_ref[...] + jnp.sum(one_hot, axis=1, keepdims=True)
        run_ref[...] = run
        cnt_ref[...] = run.astype(I32)


def rowmm_ln_route_call(a, w, l, x, g, b, wr_t, br, tm, tk):
    m, k = a.shape
    nk, nt = k // tk, m // tm
    vec = pl.BlockSpec((None, 1, D_MODEL), lambda i, kk: (l, 0, 0))
    row = pl.BlockSpec((tm, D_MODEL), lambda i, kk: (i, 0))
    plan = pl.BlockSpec((None, 1, tm), lambda i, kk: (i, 0, 0))
    return pl.pallas_call(
        functools.partial(_rowmm_ln_route_kernel, nk=nk), grid=(nt, nk),
        in_specs=[pl.BlockSpec((tm, tk), lambda i, kk: (i, kk)),
                  pl.BlockSpec((None, tk, D_MODEL), lambda i, kk: (l, kk, 0)), row, vec, vec,
                  pl.BlockSpec((N_EXPERTS, D_MODEL), lambda i, kk: (0, 0)),
                  pl.BlockSpec((N_EXPERTS, 1), lambda i, kk: (0, 0))],
        out_specs=[pl.BlockSpec((tm, XG_WIDTH), lambda i, kk: (i, 0)), row, plan, plan,
                   pl.BlockSpec((SUBLANES, LANES), lambda i, kk: (0, 0))],
        out_shape=[_sds((m, XG_WIDTH), F32), _sds((m, D_MODEL), BF), _sds((nt, 1, tm), I32),
                   _sds((nt, 1, tm), I32), _sds((SUBLANES, LANES), I32)],
        scratch_shapes=[pltpu.VMEM((tm, D_MODEL), F32), pltpu.VMEM((SUBLANES, LANES), F32)],
        compiler_params=_cp("arbitrary", "arbitrary"),
    )(a, w, x, g, b, wr_t, br)


def _group_tiles(cnt_ref):
    cum, total = [], 0
    for g in range(N_GROUPS):
        total = total + (cnt_ref[g, 0] + MOE_TILE - 1) // MOE_TILE
        cum.append(total)
    return cum


def _perm_kernel(grp_ref, rank_ref, cnt_ref, src_ref, dst_ref, zero_ref, off_ref, sem, *, to_sorted):
    cum = _group_tiles(cnt_ref)
    off_ref[0] = 0
    for g in range(1, N_GROUPS):
        off_ref[g] = cum[g - 1] * MOE_TILE

    def row_copy(s, d):
        return pltpu.make_async_copy(src_ref.at[pl.ds(s, 1)], dst_ref.at[pl.ds(d, 1)], sem)

    def wait_one():
        row_copy(0, 0).wait()

    def move(t, carry):
        pos = off_ref[grp_ref[t]] + rank_ref[t]

        @pl.when(t >= DMA_RING)
        def _():
            wait_one()

        if to_sorted:
            row_copy(t, pos).start()
        else:
            row_copy(pos, t).start()
        return carry

    lax.fori_loop(0, M_PROMPT, move, 0)
    lax.fori_loop(0, DMA_RING, lambda t, c: (wait_one(), c)[1], 0)

    if to_sorted:
        zero_ref[...] = jnp.zeros_like(zero_ref)

        def zero_copy(d):
            return pltpu.make_async_copy(zero_ref.at[pl.ds(0, 1)], dst_ref.at[pl.ds(d, 1)], sem)

        for g in range(N_GROUPS):
            lo = off_ref[g] + cnt_ref[g, 0]
            hi = cum[g] * MOE_TILE
            lax.fori_loop(lo, hi, lambda d, c: (zero_copy(d).start(), c)[1], 0)
            lax.fori_loop(lo, hi, lambda d, c: (zero_copy(d).wait(), c)[1], 0)


def perm_call(grp, rank, cnt, src, *, to_sorted):
    width = src.shape[1]
    rows = MOE_ROWS if to_sorted else M_PROMPT
    grid_spec = pltpu.PrefetchScalarGridSpec(
        num_scalar_prefetch=3, grid=(1,),
        in_specs=[pl.BlockSpec(memory_space=pl.ANY)], out_specs=pl.BlockSpec(memory_space=pl.ANY),
        scratch_shapes=[pltpu.VMEM((SUBLANES, width), F32), pltpu.SMEM((N_GROUPS,), I32),
                        pltpu.SemaphoreType.DMA(())])
    return pl.pallas_call(
        functools.partial(_perm_kernel, to_sorted=to_sorted), grid_spec=grid_spec,
        out_shape=_sds((rows, width), F32),
        compiler_params=pltpu.CompilerParams(dimension_semantics=("arbitrary",), has_side_effects=True),
    )(grp, rank, cnt, src)


def _gffn_locate(i, cnt_ref):
    cum = _group_tiles(cnt_ref)
    grp = (i >= cum[0]).astype(I32) + (i >= cum[1]).astype(I32) + (i >= cum[2]).astype(I32)
    return jnp.minimum(grp, N_GROUPS - 1), i < cum[N_GROUPS - 1], cum[N_GROUPS - 1]


def _gffn_kernel(cnt_ref, x_ref, gate_ref, wg_ref, wu_ref, wd_ref, g_ref, b_ref, o_ref, xb_ref, acc_ref, *, nc):
    i, e, c = pl.program_id(0), pl.program_id(1), pl.program_id(2)
    grp, used, _ = _gffn_locate(i, cnt_ref)
    first = (e == 0) & (c == 0)
    last = (e == PER_GROUP - 1) & (c == nc - 1)

    @pl.when(used & first)
    def _():
        xb_ref[...] = x_ref[...].astype(BF)
        acc_ref[...] = jnp.zeros_like(acc_ref)

    @pl.when(used)
    def _():
        xb = xb_ref[...]
        a = jnp.dot(xb, wg_ref[...], preferred_element_type=F32)
        u = jnp.dot(xb, wu_ref[...], preferred_element_type=F32)
        gcol = _gate_column(gate_ref[...], grp * PER_GROUP + e)
        h = ((a * _sigmoid(a)) * u * gcol).astype(BF)
        acc_ref[...] += jnp.dot(h, wd_ref[...], preferred_element_type=F32)

    @pl.when(used & last)
    def _():
        o_ref[...] = _layer_norm(DN_ALPHA * x_ref[...] + acc_ref[...], g_ref[...], b_ref[...])

    @pl.when(jnp.logical_not(used) & last)
    def _():
        o_ref[...] = jnp.zeros_like(o_ref)


def gffn_call(cnt, xs, l, wg, wu, wd, g, b):
    tf = 512
    nc = D_EXPERT // tf
    nt = MOE_ROWS // MOE_TILE

    def tile(i, cnt_ref):
        _, used, _ = _gffn_locate(i, cnt_ref)
        return jnp.where(used, i, 0)

    def expert_chunk(i, e, c, cnt_ref):
        grp, used, n_used = _gffn_locate(i, cnt_ref)
        last_grp, _, _ = _gffn_locate(jnp.maximum(n_used - 1, 0), cnt_ref)
        return (jnp.where(used, grp * PER_GROUP + e, last_grp * PER_GROUP + PER_GROUP - 1),
                jnp.where(used, c, nc - 1))

    def up_map(i, e, c, cnt_ref):
        ex, ch = expert_chunk(i, e, c, cnt_ref)
        return (l, ex, 0, ch)

    def down_map(i, e, c, cnt_ref):
        ex, ch = expert_chunk(i, e, c, cnt_ref)
        return (l, ex, ch, 0)

    vec = pl.BlockSpec((None, 1, D_MODEL), lambda i, e, c, cnt_ref: (l, 0, 0))
    grid_spec = pltpu.PrefetchScalarGridSpec(
        num_scalar_prefetch=1, grid=(nt, PER_GROUP, nc),
        in_specs=[pl.BlockSpec((MOE_TILE, D_MODEL), lambda i, e, c, cnt_ref: (tile(i, cnt_ref), 0)),
                  pl.BlockSpec((MOE_TILE, LANES), lambda i, e, c, cnt_ref: (tile(i, cnt_ref), D_MODEL // LANES)),
                  pl.BlockSpec((None, None, D_MODEL, tf), up_map),
                  pl.BlockSpec((None, None, D_MODEL, tf), up_map),
                  pl.BlockSpec((None, None, tf, D_MODEL), down_map), vec, vec],
        out_specs=pl.BlockSpec((MOE_TILE, D_MODEL), lambda i, e, c, cnt_ref: (i, 0)),
        scratch_shapes=[pltpu.VMEM((MOE_TILE, D_MODEL), BF), pltpu.VMEM((MOE_TILE, D_MODEL), F32)])
    return pl.pallas_call(
        functools.partial(_gffn_kernel, nc=nc), grid_spec=grid_spec,
        out_shape=_sds((MOE_ROWS, D_MODEL), F32),
        compiler_params=_cp("arbitrary", "arbitrary", "arbitrary"),
    )(cnt, xs, xs, wg, wu, wd, g, b)


def _order_key(score):
    bits = pltpu.bitcast(score, I32)
    return jnp.where(bits < 0, (bits ^ 0x7FFFFFFF) + 1, bits)


def _kth_largest_key(count_ge, shape):
    def body(i, ans):
        cand = ans + jnp.left_shift(jnp.int32(1), 31 - i)
        return jnp.where(count_ge(cand) >= N_SEL, cand, ans)
    return lax.fori_loop(0, 32, body, jnp.full(shape, INT_MIN, I32))


def _strict_upper(n):
    return (lax.broadcasted_iota(I32, (n, n), 0) < lax.broadcasted_iota(I32, (n, n), 1)).astype(BF)


def _strict_lower(n):
    return (lax.broadcasted_iota(I32, (n, n), 1) < lax.broadcasted_iota(I32, (n, n), 0)).astype(BF)


def _pidx_kernel(iq_ref, iw_ref, ik_ref, bias_ref, sc_ref, key_ref, *, tq, t_len):
    qi = pl.program_id(1)
    ik = ik_ref[...]
    w = _bf_round(iw_ref[...] * (IDX_HEADS ** -0.5))
    for h in range(IDX_HEADS):
        d = lax.dot_general(iq_ref[:, h * IDX_DIM:(h + 1) * IDX_DIM], ik, NT, preferred_element_type=F32)
        t = _bf_round(jnp.maximum(d * (IDX_DIM ** -0.5), 0.0)) * w[:, h:h + 1]
        if h == 0:
            sc_ref[...] = t
        else:
            sc_ref[...] += t
    row = qi * tq + lax.broadcasted_iota(I32, (tq, t_len), 0)
    col = lax.broadcasted_iota(I32, (tq, t_len), 1)
    key_ref[...] = jnp.where(col <= row, _order_key(sc_ref[...]), INT_MIN)

    def count_ge(cand):
        return jnp.sum((key_ref[...] >= cand).astype(I32), axis=1, keepdims=True)

    thr = _kth_largest_key(count_ge, (tq, 1))
    key = key_ref[...]
    gt = key > thr
    eq = key == thr
    need = N_SEL - jnp.sum(gt.astype(I32), axis=1, keepdims=True)
    n_eq = jnp.sum(eq.astype(I32), axis=1, keepdims=True)
    causal = key > INT_MIN
    bias_ref[...] = jnp.where((gt | eq) & causal, 0.0, NEG).astype(BF)
    tie = ((n_eq > need) & (thr > INT_MIN)).astype(I32)

    @pl.when(jnp.max(tie) > 0)
    def _():
        cw = 256
        upper = _strict_upper(cw)
        run = jnp.zeros((tq, 1), F32)
        need_f = need.astype(F32)
        for c in range(t_len // cw):
            kc = key_ref[:, c * cw:(c + 1) * cw]
            eq_c = kc == thr
            eq_f = jnp.where(eq_c, 1.0, 0.0)
            rank = run + jnp.dot(eq_f.astype(BF), upper, preferred_element_type=F32)
            keep = ((kc > thr) | (eq_c & (rank < need_f))) & (kc > INT_MIN)
            bias_ref[:, c * cw:(c + 1) * cw] = jnp.where(keep, 0.0, NEG).astype(BF)
            run = run + jnp.sum(eq_f, axis=1, keepdims=True)


def prompt_index_call(iq_b, iw, ik_b, tq):
    nq = SEQ // tq
    return pl.pallas_call(
        functools.partial(_pidx_kernel, tq=tq, t_len=SEQ), grid=(BATCH, nq),
        in_specs=[pl.BlockSpec((tq, IDX_HEADS * IDX_DIM), lambda b, q: (b * nq + q, 0)),
                  pl.BlockSpec((tq, IDX_HEADS), lambda b, q: (b * nq + q, 0)),
                  pl.BlockSpec((SEQ, IDX_DIM), lambda b, q: (b, 0))],
        out_specs=pl.BlockSpec((tq, SEQ), lambda b, q: (b * nq + q, 0)),
        out_shape=_sds((M_PROMPT, SEQ), BF),
        scratch_shapes=[pltpu.VMEM((tq, SEQ), F32), pltpu.VMEM((tq, SEQ), I32)],
        compiler_params=_cp("parallel", "arbitrary"),
    )(iq_b, iw, ik_b)


def _pattn_kernel(q_ref, k_ref, v_ref, bias_ref, o_ref):
    s = lax.dot_general(q_ref[...], k_ref[...], NT, preferred_element_type=F32) * (HEAD_DIM ** -0.5)
    s = s + bias_ref[...].astype(F32)
    p = jnp.exp(s - jnp.max(s, axis=1, keepdims=True))
    p = p / jnp.sum(p, axis=1, keepdims=True)
    o_ref[...] = jnp.dot(p.astype(BF), v_ref[...], preferred_element_type=F32).astype(BF)


def prompt_attn_call(q_b, k_b, v_b, bias, tq):
    nq = SEQ // tq
    qspec = pl.BlockSpec((tq, HEAD_DIM), lambda b, h, q: (b * nq + q, h))
    kspec = pl.BlockSpec((SEQ, HEAD_DIM), lambda b, h, q: (b, h))
    return pl.pallas_call(
        _pattn_kernel, grid=(BATCH, N_HEADS, nq),
        in_specs=[qspec, kspec, kspec, pl.BlockSpec((tq, SEQ), lambda b, h, q: (b * nq + q, 0))],
        out_specs=qspec, out_shape=_sds((M_PROMPT, ATTN_WIDTH), BF),
        compiler_params=_cp("parallel", "parallel", "arbitrary"),
    )(q_b, k_b, v_b, bias)


HALO = 32


def _pconv_kernel(u_ref, halo_ref, w_ref, bdw_ref, g_ref, b_ref, o_ref, ext_ref, y_ref, *, tt):
    t = pl.program_id(1)
    ext_ref[0:HALO, :] = _bf_round(jnp.where(t == 0, 0.0, halo_ref[...]))
    ext_ref[HALO:HALO + tt, :] = _bf_round(u_ref[...])
    rb_n, cb_n = 64, 256
    first = HALO - (CONV_WIDTH - 1)
    for cc in range(D_CONV // cb_n):
        cs = slice(cc * cb_n, (cc + 1) * cb_n)
        wc = _bf_round(w_ref[:, cs])
        for rb in range(tt // rb_n):
            acc = jnp.zeros((rb_n, cb_n), F32)
            for j in range(CONV_WIDTH):
                acc = acc + ext_ref[rb * rb_n + first + j:rb * rb_n + first + j + rb_n, cs] * wc[j:j + 1, :]
            y_ref[rb * rb_n:(rb + 1) * rb_n, cs] = acc + bdw_ref[:, cs]
    y = _layer_norm(y_ref[...], g_ref[...], b_ref[...])
    o_ref[...] = (y * _sigmoid(y)).astype(BF)


def prompt_conv_call(u, l, w_dw, b_dw, g, b, tt):
    nt = SEQ // tt
    per = tt // HALO
    vec = pl.BlockSpec((None, 1, D_CONV), lambda bb, t: (l, 0, 0))
    return pl.pallas_call(
        functools.partial(_pconv_kernel, tt=tt), grid=(BATCH, nt),
        in_specs=[pl.BlockSpec((tt, D_CONV), lambda bb, t: (bb * nt + t, 0)),
                  pl.BlockSpec((HALO, D_CONV), lambda bb, t: (jnp.maximum((bb * nt + t) * per - 1, 0), 0)),
                  pl.BlockSpec((None, CONV_WIDTH, D_CONV), lambda bb, t: (l, 0, 0)), vec, vec, vec],
        out_specs=pl.BlockSpec((tt, D_CONV), lambda bb, t: (bb * nt + t, 0)),
        out_shape=_sds((M_PROMPT, D_CONV), BF),
        scratch_shapes=[pltpu.VMEM((HALO + tt, D_CONV), F32), pltpu.VMEM((tt, D_CONV), F32)],
        compiler_params=_cp("parallel", "arbitrary"),
    )(u, u, w_dw, b_dw, g, b)


def _sconv_kernel(u_ref, st_ref, w_ref, bdw_ref, g_ref, b_ref, o_ref):
    w = w_ref[...]
    y = jnp.sum(st_ref[...] * w[None, :CONV_WIDTH - 1, :], axis=1)
    y = y + u_ref[0:DEC_BATCH, :] * w[CONV_WIDTH - 1:CONV_WIDTH, :] + bdw_ref[...]
    y = _layer_norm(y, g_ref[...], b_ref[...])
    y = y * _sigmoid(y)
    o_ref[...] = jnp.concatenate([y, jnp.zeros((M_SAMPLE - DEC_BATCH, D_CONV), F32)], axis=0).astype(BF)


def sample_conv_call(u, state, l, w_dw, b_dw, g, b):
    vec = pl.BlockSpec((None, 1, D_CONV), lambda i: (l, 0, 0))
    return pl.pallas_call(
        _sconv_kernel, grid=(1,),
        in_specs=[pl.BlockSpec((M_SAMPLE, D_CONV), lambda i: (0, 0)),
                  pl.BlockSpec((None, DEC_BATCH, CONV_WIDTH - 1, D_CONV), lambda i: (l, 0, 0, 0)),
                  pl.BlockSpec((None, CONV_WIDTH, D_CONV), lambda i: (l, 0, 0)), vec, vec, vec],
        out_specs=pl.BlockSpec((M_SAMPLE, D_CONV), lambda i: (0, 0)),
        out_shape=_sds((M_SAMPLE, D_CONV), BF),
        compiler_params=_cp("arbitrary"),
    )(u, state, w_dw, b_dw, g, b)


def _sidx_kernel(pt_ref, iq_ref, iw_ref, ikn_ref, *rest):
    page_refs, (sel_ref, sc_ref) = rest[:PAGES_PER_STEP], rest[PAGES_PER_STEP:]
    p = pl.program_id(1)
    w = iw_ref[...] * (IDX_HEADS ** -0.5)
    iq = iq_ref[...]

    def head_sum(d):
        return jnp.sum(jnp.maximum(d * (IDX_DIM ** -0.5), 0.0) * w, axis=0, keepdims=True)

    for j in range(PAGES_PER_STEP):
        d = lax.dot_general(iq, page_refs[j][...].astype(BF), NT, preferred_element_type=F32)
        sc_ref[pl.ds(p * PAGES_PER_STEP + j, 1), :] = head_sum(d)

    @pl.when(p == N_PAGES // PAGES_PER_STEP - 1)
    def _():
        sn = head_sum(jnp.sum(iq.astype(F32) * ikn_ref[...].astype(F32), axis=1, keepdims=True))
        kp = _order_key(sc_ref[...])
        kn = _order_key(sn)

        def total(mask):
            return jnp.sum(jnp.sum(jnp.where(mask, 1.0, 0.0), axis=0, keepdims=True), axis=1, keepdims=True)

        def count_ge(c):
            return (total(kp >= c) + jnp.where(kn >= c, 1.0, 0.0)).astype(I32)

        thr = _kth_largest_key(count_ge, (1, 1))
        gt, eq = kp > thr, kp == thr
        need = N_SEL - total(gt) - jnp.where(kn > thr, 1.0, 0.0)
        upper, lower = _strict_upper(PAGE_SIZE), _strict_lower(N_PAGES)

        def position_rank(flags):
            in_page = jnp.dot(flags.astype(BF), upper, preferred_element_type=F32)
            page_tot = jnp.broadcast_to(jnp.sum(flags, axis=1, keepdims=True), (N_PAGES, PAGE_SIZE))
            return in_page + jnp.dot(lower, page_tot.astype(BF), preferred_element_type=F32)

        eq_f = jnp.where(eq, 1.0, 0.0)
        keep = gt | (eq & (position_rank(eq_f) < need))
        keep_new = (kn > thr) | ((kn == thr) & (total(eq) < need))
        keep_f = jnp.where(keep, 1.0, 0.0)
        slot = position_rank(keep_f)
        pos = (lax.broadcasted_iota(I32, (N_PAGES, PAGE_SIZE), 0) * PAGE_SIZE
               + lax.broadcasted_iota(I32, (N_PAGES, PAGE_SIZE), 1)).astype(F32)
        want = lax.broadcasted_iota(I32, (N_SEL, PAGE_SIZE), 0).astype(F32)
        found = jnp.full((N_SEL, PAGE_SIZE), -1.0, F32)
        for pg in range(N_PAGES):
            hit = (slot[pg:pg + 1, :] == want) & keep[pg:pg + 1, :]
            found = jnp.maximum(found, jnp.where(hit, pos[pg:pg + 1, :], -1.0))
        chosen = jnp.max(found, axis=1, keepdims=True)
        new_slot = (lax.broadcasted_iota(I32, (N_SEL, 1), 0).astype(F32) == total(keep)) & keep_new
        sel_ref[...] = jnp.where(new_slot, float(PAST_LEN), chosen).astype(I32)


def sample_index_call(page_table, iq3, iw3, ikn3, cache_idx_k, l):
    def page_spec(j):
        return pl.BlockSpec((None, None, PAGE_SIZE, IDX_DIM),
                            lambda b, p, pt: (l, pt[b, p * PAGES_PER_STEP + j], 0, 0))
    grid_spec = pltpu.PrefetchScalarGridSpec(
        num_scalar_prefetch=1, grid=(DEC_BATCH, N_PAGES // PAGES_PER_STEP),
        in_specs=[pl.BlockSpec((None, IDX_HEADS, IDX_DIM), lambda b, p, pt: (b, 0, 0)),
                  pl.BlockSpec((None, IDX_HEADS, 1), lambda b, p, pt: (b, 0, 0)),
                  pl.BlockSpec((None, 1, IDX_DIM), lambda b, p, pt: (b, 0, 0))]
        + [page_spec(j) for j in range(PAGES_PER_STEP)],
        out_specs=[pl.BlockSpec((None, N_SEL, 1), lambda b, p, pt: (b, 0, 0))],
        scratch_shapes=[pltpu.VMEM((N_PAGES, PAGE_SIZE), F32)])
    sel, = pl.pallas_call(
        _sidx_kernel, grid_spec=grid_spec, out_shape=[_sds((DEC_BATCH, N_SEL, 1), I32)],
        compiler_params=_cp("parallel", "arbitrary"),
    )(page_table, iq3, iw3, ikn3, *([cache_idx_k] * PAGES_PER_STEP))
    return sel


def _sattn_kernel(sel_ref, pt_ref, q_ref, kn_ref, vn_ref, ck_ref, cv_ref, o_ref, kbuf, vbuf, sem, *, l):
    b = pl.program_id(0)

    def copies(r):
        past = jnp.minimum(sel_ref[b * N_SEL + r], PAST_LEN - 1)
        phys = pt_ref[b, past // PAGE_SIZE]
        slot = past % PAGE_SIZE
        return (pltpu.make_async_copy(ck_ref.at[l, phys, slot], kbuf.at[r], sem.at[0]),
                pltpu.make_async_copy(cv_ref.at[l, phys, slot], vbuf.at[r], sem.at[1]))

    def start(r, c):
        ck, cv = copies(r)
        ck.start()
        cv.start()
        return c

    def wait(r, c):
        ck, cv = copies(r)
        ck.wait()
        cv.wait()
        return c

    lax.fori_loop(0, N_SEL, start, 0)
    lax.fori_loop(0, N_SEL, wait, 0)

    @pl.when(sel_ref[b * N_SEL + N_SEL - 1] == PAST_LEN)
    def _():
        kbuf[N_SEL - 1] = kn_ref[...].astype(F32)
        vbuf[N_SEL - 1] = vn_ref[...].astype(F32)

    q = q_ref[...].astype(F32)
    s = jnp.sum(_bf_round(kbuf[...]) * q[None], axis=-1, keepdims=True) * (HEAD_DIM ** -0.5)
    p = jnp.exp(s - jnp.max(s, axis=0, keepdims=True))
    p = _bf_round(p / jnp.sum(p, axis=0, keepdims=True))
    pv = p * _bf_round(vbuf[...])
    half = N_SEL // 2
    o_ref[...] = (jnp.sum(pv[:half], axis=0) + jnp.sum(pv[half:], axis=0)).astype(BF)


def sample_attn_call(sel, page_table, q3, kn3, vn3, cache_k, cache_v, l):
    head = pl.BlockSpec((None, N_HEADS, HEAD_DIM), lambda b, sel_ref, pt_ref: (b, 0, 0))
    grid_spec = pltpu.PrefetchScalarGridSpec(
        num_scalar_prefetch=2, grid=(DEC_BATCH,),
        in_specs=[head, head, head, pl.BlockSpec(memory_space=pl.ANY), pl.BlockSpec(memory_space=pl.ANY)],
        out_specs=head,
        scratch_shapes=[pltpu.VMEM((N_SEL, N_HEADS, HEAD_DIM), F32), pltpu.VMEM((N_SEL, N_HEADS, HEAD_DIM), F32),
                        pltpu.SemaphoreType.DMA((2,))])
    return pl.pallas_call(
        functools.partial(_sattn_kernel, l=l), grid_spec=grid_spec,
        out_shape=_sds((DEC_BATCH, N_HEADS, HEAD_DIM), BF),
        compiler_params=_cp("arbitrary"),
    )(sel, page_table, q3, kn3, vn3, cache_k, cache_v)


def _rope_tables(pos):
    def tab(d):
        inv = ROPE_THETA ** (-jnp.arange(0, d, 2, dtype=F32) / d)
        ang = pos.astype(F32)[:, None] * inv[None, :]
        cos, sin = jnp.cos(ang), jnp.sin(ang)
        reps = LANES // d
        return (jnp.tile(jnp.concatenate([cos, cos], axis=1), (1, reps)),
                jnp.tile(jnp.concatenate([-sin, sin], axis=1), (1, reps)))
    c128, s128 = tab(HEAD_DIM)
    c64, s64 = tab(IDX_DIM)
    return c128, s128, c64, s64


def kernel(x_prompt, x_sample, cache_k, cache_v, cache_idx_k, state_conv, page_table, w_in, w_dw, b_dw,
           conv_ln_g, conv_ln_b, w_attn_out, w_conv_out, w_out, ln1_g, ln1_b, ln2_g, ln2_b, w_router, b_router,
           w_gate, w_up, w_down):
    wcat = jnp.concatenate(
        [w_in[:, :, :OFF_GA], w_in[:, :, OFF_GA + 96:], w_in[:, :, OFF_GA:OFF_GA + 96],
         jnp.zeros((DEPTH, D_MODEL, W_CAT - OFF_IKW - 96), F32)], axis=2).astype(BF)
    wao, wco, wout = w_attn_out.astype(BF), w_conv_out.astype(BF), w_out.astype(BF)
    wg, wu, wd = w_gate.astype(BF), w_up.astype(BF), w_down.astype(BF)
    wd_flat = wd.reshape(DEPTH, N_EXPERTS * D_EXPERT, D_MODEL)
    wr_t, br = w_router.T.astype(BF), b_router[:, None]
    ln1g, ln1b, ln2g, ln2b = ln1_g[:, None, :], ln1_b[:, None, :], ln2_g[:, None, :], ln2_b[:, None, :]
    b_dw3, cg3, cb3 = b_dw[:, None, :], conv_ln_g[:, None, :], conv_ln_b[:, None, :]

    tabs_p = _rope_tables(jnp.tile(jnp.arange(SEQ), BATCH))
    tabs_s = _rope_tables(jnp.full((M_SAMPLE,), PAST_LEN))

    xpf = x_prompt.reshape(M_PROMPT, D_MODEL)
    xsf = jnp.concatenate([x_sample.reshape(DEC_BATCH, D_MODEL),
                           jnp.zeros((M_SAMPLE - DEC_BATCH, D_MODEL), F32)], axis=0)
    xpb, xsb = xpf.astype(BF), xsf.astype(BF)

    outs = [[] for _ in range(8)]
    for l in range(DEPTH):
        q_b, k_f, k_b, v_f, v_b, iq_b, ik_f, ik_b, iw, u = in_proj(xpb, l, wcat, tabs_p, 1024)
        bias = prompt_index_call(iq_b, iw, ik_b, 256)
        attn_b = prompt_attn_call(q_b, k_b, v_b, bias, 256)
        conv_b = prompt_conv_call(u, l, w_dw, b_dw3, cg3, cb3, 128)
        merged = merge_call(xpb, attn_b, conv_b, l, wcat, wao, wco, 1024)
        xg, _, grp, rank, cnt = rowmm_ln_route_call(merged, wout, l, xpf, ln1g, ln1b, wr_t, br, 256, 512)
        grp, rank = grp.reshape(M_PROMPT), rank.reshape(M_PROMPT)
        x_sorted = perm_call(grp, rank, cnt, xg, to_sorted=True)
        y_sorted = gffn_call(cnt, x_sorted, l, wg, wu, wd, ln2g, ln2b)
        xpf = perm_call(grp, rank, cnt, y_sorted, to_sorted=False)
        xpb = xpf.astype(BF)
        outs[0].append(k_f.reshape(M_PROMPT // PAGE_SIZE, PAGE_SIZE, N_HEADS, HEAD_DIM))
        outs[1].append(v_f.reshape(M_PROMPT // PAGE_SIZE, PAGE_SIZE, N_HEADS, HEAD_DIM))
        outs[2].append(ik_f.reshape(M_PROMPT // PAGE_SIZE, PAGE_SIZE, IDX_DIM))
        outs[3].append(u.reshape(BATCH, SEQ, D_CONV)[:, SEQ - (CONV_WIDTH - 1):, :])

        q_b, k_f, k_b, v_f, v_b, iq_b, ik_f, ik_b, iw, u = in_proj(xsb, l, wcat, tabs_s, M_SAMPLE)
        sel = sample_index_call(
            page_table, iq_b[:DEC_BATCH].reshape(DEC_BATCH, IDX_HEADS, IDX_DIM),
            iw[:DEC_BATCH].reshape(DEC_BATCH, IDX_HEADS, 1), ik_b[:DEC_BATCH].reshape(DEC_BATCH, 1, IDX_DIM),
            cache_idx_k, l)
        heads = lambda a: a[:DEC_BATCH].reshape(DEC_BATCH, N_HEADS, HEAD_DIM)
        attn_s = sample_attn_call(sel.reshape(DEC_BATCH * N_SEL), page_table, heads(q_b), heads(k_b), heads(v_b),
                                  cache_k, cache_v, l)
        attn_b = jnp.concatenate([attn_s.reshape(DEC_BATCH, ATTN_WIDTH),
                                  jnp.zeros((M_SAMPLE - DEC_BATCH, ATTN_WIDTH), BF)], axis=0)
        conv_b = sample_conv_call(u, state_conv, l, w_dw, b_dw3, cg3, cb3)
        merged = merge_call(xsb, attn_b, conv_b, l, wcat, wao, wco, M_SAMPLE)
        xg, x1b, _, _, _ = rowmm_ln_route_call(merged, wout, l, xsf, ln1g, ln1b, wr_t, br, M_SAMPLE, 512)
        h = moe_up_call(x1b, xg, l, wg, wu, M_SAMPLE)
        xsf, xsb = rowmm_ln_call(h, wd_flat, l, xg, ln2g, ln2b, M_SAMPLE, 512)
        outs[4].append(k_f[:DEC_BATCH].reshape(DEC_BATCH, 1, N_HEADS, HEAD_DIM))
        outs[5].append(v_f[:DEC_BATCH].reshape(DEC_BATCH, 1, N_HEADS, HEAD_DIM))
        outs[6].append(ik_f[:DEC_BATCH].reshape(DEC_BATCH, 1, IDX_DIM))
        outs[7].append(jnp.concatenate([state_conv[l][:, 1:, :], u[:DEC_BATCH, None, :]], axis=1))

    y_prompt = xpf.reshape(BATCH, SEQ, D_MODEL)
    y_sample = xsf[:DEC_BATCH].reshape(DEC_BATCH, 1, D_MODEL)
    return (y_prompt, y_sample) + tuple(jnp.stack(o) for o in outs)
```
